```python
import math, functools
import jax, jax.numpy as jnp
from jax import lax
import numpy as np

D_MODEL = 1024
BATCH = 4
SEQ = 8192
DEPTH = 1

D_MIX = D_MODEL
N_ATT_HEADS = 8
HEAD_DIM = 64
D_ATT = N_ATT_HEADS * HEAD_DIM
D_POOL = D_MIX - D_ATT
POOL_WINDOWS = (2, 4, 8, 16)
N_POOL_GROUPS = len(POOL_WINDOWS)
POOL_GROUP = D_POOL // N_POOL_GROUPS
DILATED_BRANCHES = ((128, 1), (512, 4), (2048, 16))
MAX_DIL = 16
BLOCK = 128
ROPE_THETA = 10000.0
EPS = 1e-6
N_KEYS = 128
N_EXPERTS = N_KEYS * N_KEYS
PEER_HEADS = 8
PEER_KEY_DIM = 256
PEER_HALF = PEER_KEY_DIM // 2
PEER_TOPK = 16
PEER_CHUNK = 128

kernel_name = "hymba_dilated_pool_peer_block"


def _rmsnorm(x, g):
    xf = x.astype(jnp.float32)
    y = xf * lax.rsqrt(jnp.mean(xf * xf, axis=-1, keepdims=True) + EPS)
    return (y * g.astype(jnp.float32)).astype(x.dtype)


def _rope(x, pos):
    hd = x.shape[-1]
    freqs = ROPE_THETA ** (-jnp.arange(0, hd, 2, dtype=jnp.float32) / hd)
    ang = pos[:, None] * freqs[None, :]
    cos = jnp.cos(ang)[None, :, None, :]
    sin = jnp.sin(ang)[None, :, None, :]
    xf = x.astype(jnp.float32)
    x1, x2 = xf[..., : hd // 2], xf[..., hd // 2:]
    return jnp.concatenate([x1 * cos - x2 * sin, x2 * cos + x1 * sin], axis=-1)


def _dilated_branch(q, k, v, dil, n_back):
    B, H, Sp, hd = q.shape
    L = Sp // dil
    nb = L // BLOCK

    def to_blocks(t):
        t = t.reshape(B, H, L, dil, hd).transpose(0, 1, 3, 2, 4)
        return t.reshape(B, H, dil, nb, BLOCK, hd)

    def with_prev(t):
        prev = jnp.pad(t, ((0, 0), (0, 0), (0, 0), (1, 0), (0, 0), (0, 0)))[:, :, :, :nb]
        return jnp.concatenate([prev, t], axis=4)

    qb = to_blocks(q)
    kc = with_prev(to_blocks(k))
    vc = with_prev(to_blocks(v))
    s = jnp.einsum('bhrnqd,bhrnkd->bhrnqk', qb, kc)
    a = jnp.arange(BLOCK)[:, None]
    c = jnp.arange(2 * BLOCK)[None, :]
    dist = a - c + BLOCK
    kidx = jnp.arange(nb)[:, None, None] * BLOCK + c[None] - BLOCK
    valid = (dist >= 0) & (dist <= n_back) & (kidx >= 0)
    s = jnp.where(valid, s, -jnp.inf)
    m = jnp.max(s, axis=-1, keepdims=True)
    p = jnp.exp(s - m)
    l = jnp.sum(p, axis=-1, keepdims=True)
    o = jnp.einsum('bhrnqk,bhrnkd->bhrnqd', p, vc) / l
    lse = (m + jnp.log(l))[..., 0]
    o = o.reshape(B, H, dil, L, hd).transpose(0, 1, 3, 2, 4).reshape(B, H, Sp, hd)
    lse = lse.reshape(B, H, dil, L).transpose(0, 1, 3, 2).reshape(B, H, Sp)
    return o, lse


def _dilated_attention(q, k, v):
    B, H, S, hd = q.shape
    unit = MAX_DIL * BLOCK
    Sp = -(-S // unit) * unit
    pad = ((0, 0), (0, 0), (0, Sp - S), (0, 0))
    q = jnp.pad(q * (HEAD_DIM ** -0.5), pad)
    k = jnp.pad(k, pad)
    v = jnp.pad(v, pad)
    outs, lses = [], []
    for window, dil in DILATED_BRANCHES:
        o_i, lse_i = _dilated_branch(q, k, v, dil, window // dil)
        outs.append(o_i)
        lses.append(lse_i)
    wts = jax.nn.softmax(jnp.stack(lses, axis=0), axis=0)
    o = jnp.einsum('gbhs,gbhsd->bhsd', wts, jnp.stack(outs, axis=0))
    return o[:, :, :S]


def _trailing_mean_minus_self(x, w):
    S = x.shape[1]
    csum = jnp.pad(jnp.cumsum(x, axis=1), ((0, 0), (1, 0), (0, 0)))
    hi = csum[:, 1:]
    lo = jnp.pad(csum, ((0, 0), (w - 1, 0), (0, 0)))[:, :S]
    cnt = jnp.minimum(jnp.arange(S) + 1, w).astype(jnp.float32)
    return (hi - lo) / cnt[None, :, None] - x


def _pool_mixer(p, pool_w, pool_scale):
    B, S, _ = p.shape
    pg = p.astype(jnp.float32).reshape(B, S, N_POOL_GROUPS, POOL_GROUP)
    pooled = jnp.stack([_trailing_mean_minus_self(pg[:, :, g], w)
                        for g, w in enumerate(POOL_WINDOWS)], axis=2)
    y = jnp.einsum('bsgc,gcd->bsgd', pooled, pool_w.astype(jnp.float32))
    return y.reshape(B, S, D_POOL) * pool_scale.astype(jnp.float32)


def _peer(h, w_query, sub_keys, expert_u, expert_v):
    B, S, D = h.shape
    T = B * S
    ht = h.reshape(T, D)
    q = (ht @ w_query).astype(jnp.float32).reshape(T, PEER_HEADS, 2, PEER_HALF)
    s = jnp.einsum('thpc,hpnc->thpn', q, sub_keys.astype(jnp.float32))
    s_top, i_top = lax.top_k(s, PEER_TOPK)
    cand = (s_top[:, :, 0, :, None] + s_top[:, :, 1, None, :]).reshape(T, PEER_HEADS, -1)
    cand_idx = (i_top[:, :, 0, :, None] * N_KEYS + i_top[:, :, 1, None, :]).reshape(T, PEER_HEADS, -1)
    best, pos = lax.top_k(cand, PEER_TOPK)
    idx = jnp.take_along_axis(cand_idx, pos, axis=-1)
    gate = jax.nn.softmax(best, axis=-1)
    n_chunks = T // PEER_CHUNK
    xs = (ht.reshape(n_chunks, PEER_CHUNK, D),
          idx.reshape(n_chunks, PEER_CHUNK, PEER_HEADS * PEER_TOPK),
          gate.reshape(n_chunks, PEER_CHUNK, PEER_HEADS * PEER_TOPK))

    def chunk_fn(args):
        xc, ic, gc = args
        u = expert_u[ic]
        a = jnp.einsum('ckd,cd->ck', u.astype(jnp.float32), xc.astype(jnp.float32))
        act = jax.nn.gelu(a, approximate=False) * gc
        vsel = expert_v[ic]
        return jnp.einsum('ck,ckd->cd', act, vsel.astype(jnp.float32))

    y = lax.map(chunk_fn, xs)
    return y.reshape(B, S, D)


def setup_inputs(seed: int = 0) -> dict:
    key = jax.random.key(seed)
    ks = jax.random.split(key, 14)
    f32 = jnp.float32
    nrm = lambda k, shape, scale: (jax.random.normal(k, shape, f32) * scale)
    return {
        "x": nrm(ks[0], (BATCH, SEQ, D_MODEL), 1.0),
        "attn_norm_g": 1.0 + nrm(ks[1], (DEPTH, D_MODEL), 0.1),
        "w_in": nrm(ks[2], (DEPTH, D_MODEL, 3 * D_ATT + D_POOL), D_MODEL ** -0.5),
        "q_norm_g": 1.0 + nrm(ks[3], (DEPTH, HEAD_DIM), 0.1),
        "k_norm_g": 1.0 + nrm(ks[4], (DEPTH, HEAD_DIM), 0.1),
        "pool_w": nrm(ks[5], (DEPTH, N_POOL_GROUPS, POOL_GROUP, POOL_GROUP), POOL_GROUP ** -0.5),
        "pool_scale": 1.0 + nrm(ks[6], (DEPTH, D_POOL), 0.1),
        "w_out": nrm(ks[7], (DEPTH, D_MIX, D_MODEL), D_MIX ** -0.5),
        "ffn_norm_g": 1.0 + nrm(ks[8], (DEPTH, D_MODEL), 0.1),
        "peer_w_query": nrm(ks[9], (DEPTH, D_MODEL, PEER_HEADS * PEER_KEY_DIM), D_MODEL ** -0.5),
        "peer_sub_keys": nrm(ks[10], (DEPTH, PEER_HEADS, 2, N_KEYS, PEER_HALF), PEER_HALF ** -0.5),
        "peer_u": nrm(ks[11], (DEPTH, N_EXPERTS, D_MODEL), D_MODEL ** -0.5),
        "peer_v": nrm(ks[12], (DEPTH, N_EXPERTS, D_MODEL), (PEER_HEADS) ** -0.5),
    }


def reference(x, attn_norm_g, w_in, q_norm_g, k_norm_g, pool_w, pool_scale, w_out,
              ffn_norm_g, peer_w_query, peer_sub_keys, peer_u, peer_v):
    B, S, D = x.shape
    pos = jnp.arange(S, dtype=jnp.float32)
    for l in range(DEPTH):
        h = _rmsnorm(x, attn_norm_g[l])
        proj = h @ w_in[l]
        q, k, v, p = jnp.split(proj, [D_ATT, 2 * D_ATT, 3 * D_ATT], axis=-1)
        q = _rope(_rmsnorm(q.reshape(B, S, N_ATT_HEADS, HEAD_DIM), q_norm_g[l]), pos)
        k = _rope(_rmsnorm(k.reshape(B, S, N_ATT_HEADS, HEAD_DIM), k_norm_g[l]), pos)
        v = v.astype(jnp.float32).reshape(B, S, N_ATT_HEADS, HEAD_DIM)
        att = _dilated_attention(q.transpose(0, 2, 1, 3), k.transpose(0, 2, 1, 3),
                                 v.transpose(0, 2, 1, 3))
        att = att.transpose(0, 2, 1, 3).reshape(B, S, D_ATT)
        pool = _pool_mixer(p, pool_w[l], pool_scale[l])
        mixed = jnp.concatenate([att, pool], axis=-1).astype(x.dtype)
        x = x + (mixed @ w_out[l]).astype(x.dtype)
        h2 = _rmsnorm(x, ffn_norm_g[l])
        x = x + _peer(h2, peer_w_query[l], peer_sub_keys[l], peer_u[l], peer_v[l]).astype(x.dtype)
    return x
```

```python
import functools
import math

import jax
import jax.numpy as jnp
from jax import lax
from jax.experimental import pallas as pl
from jax.experimental.pallas import tpu as pltpu

N_ATT_HEADS = 8
HEAD_DIM = 64
D_ATT = N_ATT_HEADS * HEAD_DIM
POOL_WINDOWS = (2, 4, 8, 16)
POOL_GROUP = 128
POOL_HALO = 16
DILATIONS = (1, 4, 16)
BLOCK = 128
ROPE_THETA = 10000.0
EPS = 1e-6
N_KEYS = 128
PEER_HEADS = 8
PEER_HALF = 128
PEER_TOPK = 16
N_SLOTS = PEER_HEADS * PEER_TOPK

LANES = 128
SUBLANES = 8
ROW_SUBLANES = 4
HALF_D = ROW_SUBLANES * LANES
VMEM_LIMIT_TABLE = 50 * 1024 * 1024
VMEM_LIMIT_DENSE = 48 * 1024 * 1024

NEG = -1e30
BF16 = jnp.bfloat16
F32 = jnp.float32


def _nt_dot(a, b):
    return lax.dot_general(a, b, (((1,), (1,)), ((), ())), preferred_element_type=F32)


def _split_dot(x, w_bf16):
    hi = x.astype(BF16)
    lo = (x - hi.astype(F32)).astype(BF16)
    return (jnp.dot(hi, w_bf16, preferred_element_type=F32)
            + jnp.dot(lo, w_bf16, preferred_element_type=F32))


def _inproj_kernel(x_ref, g_ref, w_ref, qg_ref, kg_ref, cos_ref, sina_ref, sinb_ref, bd_ref,
                   q_ref, k_ref, v_ref, p_ref):
    x = x_ref[...]
    ms = jnp.mean(x * x, axis=-1, keepdims=True)
    h = (x * lax.rsqrt(ms + EPS) * g_ref[...]).astype(BF16)
    proj = jnp.dot(h, w_ref[...], preferred_element_type=F32)
    cos = jnp.concatenate([cos_ref[...]] * (D_ATT // LANES), axis=1)
    sina = jnp.concatenate([sina_ref[...]] * (D_ATT // LANES), axis=1)
    sinb = jnp.concatenate([sinb_ref[...]] * (D_ATT // LANES), axis=1)
    bd = bd_ref[...]

    def head_norm_rope(z, gain):
        msh = _split_dot(z * z, bd)
        z = z * lax.rsqrt(msh + EPS) * gain
        half = HEAD_DIM // 2
        return (z * cos + pltpu.roll(z, half, 1) * sina
                + pltpu.roll(z, D_ATT - half, 1) * sinb)

    q_ref[...] = head_norm_rope(proj[:, :D_ATT], qg_ref[...]).astype(BF16)
    k_ref[...] = head_norm_rope(proj[:, D_ATT:2 * D_ATT], kg_ref[...]).astype(BF16)
    v_ref[...] = proj[:, 2 * D_ATT:3 * D_ATT].astype(BF16)
    p_ref[...] = proj[:, 3 * D_ATT:]


def _inproj(x2, g, w_bf, qg, kg, cos, sina, sinb, bd, seq, tm):
    T, D = x2.shape
    n_out = w_bf.shape[1]
    d_pool = n_out - 3 * D_ATT
    tiles_per_seq = seq // tm
    row = lambda i: (i, 0)
    fixed = lambda i: (0, 0)
    pos = lambda i: (i % tiles_per_seq, 0)
    return pl.pallas_call(
        _inproj_kernel,
        grid=(T // tm,),
        in_specs=[
            pl.BlockSpec((tm, D), row),
            pl.BlockSpec((1, D), fixed),
            pl.BlockSpec((D, n_out), fixed),
            pl.BlockSpec((1, D_ATT), fixed),
            pl.BlockSpec((1, D_ATT), fixed),
            pl.BlockSpec((tm, LANES), pos),
            pl.BlockSpec((tm, LANES), pos),
            pl.BlockSpec((tm, LANES), pos),
            pl.BlockSpec((D_ATT, D_ATT), fixed),
        ],
        out_specs=[
            pl.BlockSpec((tm, D_ATT), row),
            pl.BlockSpec((tm, D_ATT), row),
            pl.BlockSpec((tm, D_ATT), row),
            pl.BlockSpec((tm, d_pool), row),
        ],
        out_shape=[
            jax.ShapeDtypeStruct((T, D_ATT), BF16),
            jax.ShapeDtypeStruct((T, D_ATT), BF16),
            jax.ShapeDtypeStruct((T, D_ATT), BF16),
            jax.ShapeDtypeStruct((T, d_pool), F32),
        ],
        compiler_params=pltpu.CompilerParams(vmem_limit_bytes=VMEM_LIMIT_DENSE),
        name="inproj",
    )(x2, g, w_bf, qg, kg, cos, sina, sinb, bd)


def _attn_kernel(q_ref, kp_ref, kc_ref, vp_ref, vc_ref, o_ref, l_ref, *, rq):
    n = pl.program_id(2)
    lane = lax.broadcasted_iota(jnp.int32, (BLOCK, LANES), 1)
    a = lax.broadcasted_iota(jnp.int32, (BLOCK, 2 * BLOCK), 0)
    c = lax.broadcasted_iota(jnp.int32, (BLOCK, 2 * BLOCK), 1)
    band = (c >= a) & (c <= a + BLOCK)
    has_prev = (n > 0) | (c >= BLOCK)
    zero = jnp.zeros((), BF16)
    for i in range(rq // BLOCK):
        q = q_ref[i * BLOCK:(i + 1) * BLOCK, :]
        if i == 0:
            keys = jnp.concatenate([kp_ref[...], kc_ref[0:BLOCK, :]], axis=0)
            vals = jnp.concatenate([vp_ref[...], vc_ref[0:BLOCK, :]], axis=0)
            valid = band & has_prev
        else:
            keys = kc_ref[(i - 1) * BLOCK:(i + 1) * BLOCK, :]
            vals = vc_ref[(i - 1) * BLOCK:(i + 1) * BLOCK, :]
            valid = band
        outs, lses = [], []
        for hh in range(2):
            in_head = (lane >= hh * HEAD_DIM) & (lane < (hh + 1) * HEAD_DIM)
            s = _nt_dot(jnp.where(in_head, q, zero), keys)
            s = jnp.where(valid, s, NEG)
            m = jnp.max(s, axis=1, keepdims=True)
            p = jnp.exp(s - m)
            l = jnp.sum(p, axis=1, keepdims=True)
            o = jnp.dot(p.astype(BF16), vals, preferred_element_type=F32)
            outs.append(o / l)
            lses.append(jnp.broadcast_to(m + jnp.log(l), (BLOCK, LANES)))
        first = lane < HEAD_DIM
        o_ref[i * BLOCK:(i + 1) * BLOCK, :] = jnp.where(first, outs[0], outs[1])
        l_ref[i * BLOCK:(i + 1) * BLOCK, :] = jnp.where(first, lses[0], lses[1])


def _attn_branch(q, k, v, batch, seq, dil):
    L = seq // dil
    width = dil * D_ATT
    rq = min(4 * BLOCK, L)
    nq = L // rq
    blocks_per_rq = rq // BLOCK
    view = lambda t: t.reshape(batch, L, width)
    cur = lambda b, j, n: (b, n, j)
    prev = lambda b, j, n: (b, jnp.maximum(n * blocks_per_rq - 1, 0), j)
    o, lse = pl.pallas_call(
        functools.partial(_attn_kernel, rq=rq),
        grid=(batch, width // LANES, nq),
        in_specs=[
            pl.BlockSpec((None, rq, LANES), cur),
            pl.BlockSpec((None, BLOCK, LANES), prev),
            pl.BlockSpec((None, rq, LANES), cur),
            pl.BlockSpec((None, BLOCK, LANES), prev),
            pl.BlockSpec((None, rq, LANES), cur),
        ],
        out_specs=[
            pl.BlockSpec((None, rq, LANES), cur),
            pl.BlockSpec((None, rq, LANES), cur),
        ],
        out_shape=[
            jax.ShapeDtypeStruct((batch, L, width), F32),
            jax.ShapeDtypeStruct((batch, L, width), F32),
        ],
        name=f"attn_d{dil}",
    )(view(q), view(k), view(k), view(v), view(v))
    return o.reshape(batch * seq, D_ATT), lse.reshape(batch * seq, D_ATT)


def _outproj_kernel(x_ref, o1_ref, o2_ref, o3_ref, l1_ref, l2_ref, l3_ref, p_ref, halo_ref,
                    pw_ref, ps_ref, wo_ref, g2_ref, wq_ref,
                    x1_ref, h2_ref, qp_ref, buf_ref, *, tm, tiles_per_seq):
    i = pl.program_id(0)
    l1, l2, l3 = l1_ref[...], l2_ref[...], l3_ref[...]
    m = jnp.maximum(jnp.maximum(l1, l2), l3)
    w1, w2, w3 = jnp.exp(l1 - m), jnp.exp(l2 - m), jnp.exp(l3 - m)
    att = (w1 * o1_ref[...] + w2 * o2_ref[...] + w3 * o3_ref[...]) / (w1 + w2 + w3)

    seq_tile = i % tiles_per_seq
    buf_ref[0:POOL_HALO, :] = jnp.where(seq_tile > 0, halo_ref[...], 0.0)
    buf_ref[POOL_HALO:POOL_HALO + tm, :] = p_ref[...]
    pos = seq_tile * tm + lax.broadcasted_iota(jnp.int32, (tm, 1), 0)
    pooled = []
    for g, w in enumerate(POOL_WINDOWS):
        cols = slice(g * POOL_GROUP, (g + 1) * POOL_GROUP)
        acc = buf_ref[POOL_HALO:POOL_HALO + tm, cols]
        for j in range(1, w):
            acc = acc + buf_ref[POOL_HALO - j:POOL_HALO - j + tm, cols]
        cnt = jnp.minimum(pos + 1, w).astype(F32)
        centred = acc / cnt - buf_ref[POOL_HALO:POOL_HALO + tm, cols]
        pooled.append(jnp.dot(centred.astype(BF16), pw_ref[g], preferred_element_type=F32))
    pool = jnp.concatenate(pooled, axis=1) * ps_ref[...]

    mixed = jnp.concatenate([att, pool], axis=1).astype(BF16)
    x1 = x_ref[...] + jnp.dot(mixed, wo_ref[...], preferred_element_type=F32)
    x1_ref[...] = x1
    ms = jnp.mean(x1 * x1, axis=-1, keepdims=True)
    h2 = x1 * lax.rsqrt(ms + EPS) * g2_ref[...]
    h2_ref[...] = h2
    qp_ref[...] = jnp.dot(h2.astype(BF16), wq_ref[...], preferred_element_type=F32).astype(BF16)


def _outproj(x2, o1, o2, o3, l1, l2, l3, p, pw_bf, ps, wo_bf, g2, wq_bf, seq, tm):
    T, D = x2.shape
    d_pool = p.shape[1]
    nq = wq_bf.shape[1]
    tiles_per_seq = seq // tm
    row = lambda i: (i, 0)
    fixed = lambda i: (0, 0)
    halo = lambda i: (jnp.maximum(i * (tm // POOL_HALO) - 1, 0), 0)
    return pl.pallas_call(
        functools.partial(_outproj_kernel, tm=tm, tiles_per_seq=tiles_per_seq),
        grid=(T // tm,),
        in_specs=[pl.BlockSpec((tm, D), row)]
        + [pl.BlockSpec((tm, D_ATT), row)] * 6
        + [
            pl.BlockSpec((tm, d_pool), row),
            pl.BlockSpec((POOL_HALO, d_pool), halo),
            pl.BlockSpec((len(POOL_WINDOWS), POOL_GROUP, POOL_GROUP), lambda i: (0, 0, 0)),
            pl.BlockSpec((1, d_pool), fixed),
            pl.BlockSpec((D_ATT + d_pool, D), fixed),
            pl.BlockSpec((1, D), fixed),
            pl.BlockSpec((D, nq), fixed),
        ],
        out_specs=[
            pl.BlockSpec((tm, D), row),
            pl.BlockSpec((tm, D), row),
            pl.BlockSpec((tm, nq), row),
        ],
        out_shape=[
            jax.ShapeDtypeStruct((T, D), F32),
            jax.ShapeDtypeStruct((T, D), F32),
            jax.ShapeDtypeStruct((T, nq), BF16),
        ],
        scratch_shapes=[pltpu.VMEM((POOL_HALO + tm, d_pool), F32)],
        compiler_params=pltpu.CompilerParams(vmem_limit_bytes=VMEM_LIMIT_DENSE),
        name="outproj",
    )(x2, o1, o2, o3, l1, l2, l3, p, p, pw_bf, ps, wo_bf, g2, wq_bf)


def _take_top(vals, n_take):
    rows = vals.shape[0]
    row_id = lax.broadcasted_iota(jnp.int32, vals.shape, 0)
    top_v, top_i = [], []
    for _ in range(n_take):
        m = jnp.max(vals, axis=0, keepdims=True)
        pos = jnp.min(jnp.where(vals == m, row_id, rows), axis=0, keepdims=True)
        top_v.append(m)
        top_i.append(pos)
        vals = jnp.where(row_id == pos, -jnp.inf, vals)
    return jnp.concatenate(top_v, axis=0), jnp.concatenate(top_i, axis=0)


def _select_rows(table, pos):
    r = table.shape[0]
    row_id = lax.broadcasted_iota(jnp.int32, table.shape, 0)
    out = []
    for j in range(pos.shape[0]):
        hit = row_id == pos[j:j + 1, :]
        out.append(jnp.sum(jnp.where(hit, table, 0), axis=0, keepdims=True))
    return jnp.concatenate(out, axis=0)


def _topk_kernel(qp_ref, keys_ref, idx_ref, gate_ref):
    idx_rows, gate_rows = [], []
    for h in range(PEER_HEADS):
        tops = []
        for half in range(2):
            col = (2 * h + half) * PEER_HALF
            s = _nt_dot(keys_ref[2 * h + half], qp_ref[:, col:col + PEER_HALF])
            tops.append(_take_top(s, PEER_TOPK))
        (s1, i1), (s2, i2) = tops
        cand = jnp.concatenate([s1[a:a + 1, :] + s2 for a in range(PEER_TOPK)], axis=0)
        best, pos = _take_top(cand, PEER_TOPK)
        expert = (_select_rows(i1, pos >> 4) * N_KEYS
                  + _select_rows(i2, pos & (PEER_TOPK - 1)))
        e = jnp.exp(best - best[0:1, :])
        gate_rows.append(e / jnp.sum(e, axis=0, keepdims=True))
        idx_rows.append(expert * ROW_SUBLANES)
    idx_ref[...] = jnp.concatenate(idx_rows, axis=0).T
    gate_ref[...] = jnp.concatenate(gate_rows, axis=0).T


def _topk(qp, keys_bf, tq):
    T, nq = qp.shape
    return pl.pallas_call(
        _topk_kernel,
        grid=(T // tq,),
        in_specs=[
            pl.BlockSpec((tq, nq), lambda i: (i, 0)),
            pl.BlockSpec((2 * PEER_HEADS, N_KEYS, PEER_HALF), lambda i: (0, 0, 0)),
        ],
        out_specs=[
            pl.BlockSpec((tq, N_SLOTS), lambda i: (i, 0)),
            pl.BlockSpec((tq, N_SLOTS), lambda i: (i, 0)),
        ],
        out_shape=[
            jax.ShapeDtypeStruct((T, N_SLOTS), jnp.int32),
            jax.ShapeDtypeStruct((T, N_SLOTS), F32),
        ],
        name="topk",
    )(qp, keys_bf)


def _pack_table(w):
    n, d = w.shape
    bits = lax.bitcast_convert_type(w.astype(BF16), jnp.uint16).astype(jnp.uint32)
    words = bits[:, :d // 2] | (bits[:, d // 2:] << 16)
    return words.reshape(n * ROW_SUBLANES, LANES)


def _unpack(words):
    lo = pltpu.bitcast(words << 16, F32)
    hi = pltpu.bitcast(words & jnp.uint32(0xFFFF0000), F32)
    return lo, hi


def _row_to_tile(row, start):
    return jnp.concatenate(
        [row[:, start + LANES * j:start + LANES * (j + 1)] for j in range(ROW_SUBLANES)], axis=0)


def _gelu(a):
    return 0.5 * a * (1.0 + lax.erf(a * (1.0 / math.sqrt(2.0))))


def _peer_u_kernel(idx_ref, gate_ref, h_ref, tab_ref, act_ref, pa_ref, pb_ref, *, tm):
    rows = N_SLOTS * ROW_SUBLANES
    r = lax.broadcasted_iota(jnp.int32, (rows, LANES), 0)
    c = lax.broadcasted_iota(jnp.int32, (rows, LANES), 1)
    slot_of_row = (r >> 2) == c
    ones = jnp.ones((LANES, LANES), BF16)

    def products(t, dst_ref):
        hrow = h_ref[pl.ds(t, 1), :]
        h_lo = _row_to_tile(hrow, 0)
        h_hi = _row_to_tile(hrow, HALF_D)
        for k in range(N_SLOTS):
            e = pl.multiple_of(idx_ref[t, k], ROW_SUBLANES)
            lo, hi = _unpack(tab_ref[pl.ds(e, ROW_SUBLANES), :])
            dst_ref[ROW_SUBLANES * k:ROW_SUBLANES * (k + 1), :] = lo * h_lo + hi * h_hi

    def finish(t, src_ref):
        y = _split_dot(src_ref[...], ones)
        a = jnp.sum(jnp.where(slot_of_row, y, 0.0), axis=0, keepdims=True)
        act_ref[pl.ds(t, 1), :] = _gelu(a) * gate_ref[pl.ds(t, 1), :]

    products(0, pa_ref)

    def body(i, carry):
        t = 2 * i
        products(t + 1, pb_ref)
        finish(t, pa_ref)
        products(t + 2, pa_ref)
        finish(t + 1, pb_ref)
        return carry

    lax.fori_loop(0, tm // 2 - 1, body, 0)
    products(tm - 1, pb_ref)
    finish(tm - 2, pa_ref)
    finish(tm - 1, pb_ref)


def _peer_u(idx, gate, h2, tab, tm):
    T, D = h2.shape
    return pl.pallas_call(
        functools.partial(_peer_u_kernel, tm=tm),
        grid=(T // tm,),
        in_specs=[
            pl.BlockSpec((tm, N_SLOTS), lambda i: (i, 0), memory_space=pltpu.SMEM),
            pl.BlockSpec((tm, N_SLOTS), lambda i: (i, 0)),
            pl.BlockSpec((tm, D), lambda i: (i, 0)),
            pl.BlockSpec(tab.shape, lambda i: (0, 0), pipeline_mode=pl.Buffered(1)),
        ],
        out_specs=pl.BlockSpec((tm, N_SLOTS), lambda i: (i, 0)),
        out_shape=jax.ShapeDtypeStruct((T, N_SLOTS), F32),
        scratch_shapes=[pltpu.VMEM((N_SLOTS * ROW_SUBLANES, LANES), F32)] * 2,
        compiler_params=pltpu.CompilerParams(vmem_limit_bytes=VMEM_LIMIT_TABLE),
        name="peer_u",
    )(idx, gate, h2, tab)


def _peer_v_kernel(idx_ref, act_ref, x_ref, tab_ref, out_ref, *, tm):
    n_acc = 4

    def token(t, carry):
        acc_lo = [jnp.zeros((ROW_SUBLANES, LANES), F32) for _ in range(n_acc)]
        acc_hi = [jnp.zeros((ROW_SUBLANES, LANES), F32) for _ in range(n_acc)]
        for k in range(N_SLOTS):
            e = pl.multiple_of(idx_ref[t, k], ROW_SUBLANES)
            lo, hi = _unpack(tab_ref[pl.ds(e, ROW_SUBLANES), :])
            a = act_ref[t, k]
            acc_lo[k % n_acc] = acc_lo[k % n_acc] + a * lo
            acc_hi[k % n_acc] = acc_hi[k % n_acc] + a * hi
        y_lo = (acc_lo[0] + acc_lo[1]) + (acc_lo[2] + acc_lo[3])
        y_hi = (acc_hi[0] + acc_hi[1]) + (acc_hi[2] + acc_hi[3])
        y = jnp.concatenate([y_lo[j:j + 1, :] for j in range(ROW_SUBLANES)]
                            + [y_hi[j:j + 1, :] for j in range(ROW_SUBLANES)], axis=1)
        out_ref[pl.ds(t, 1), :] = x_ref[pl.ds(t, 1), :] + y
        return carry

    lax.fori_loop(0, tm, token, 0)


def _peer_v(idx, act, x1, tab, tm):
    T, D = x1.shape
    return pl.pallas_call(
        functools.partial(_peer_v_kernel, tm=tm),
        grid=(T // tm,),
        in_specs=[
            pl.BlockSpec((tm, N_SLOTS), lambda i: (i, 0), memory_space=pltpu.SMEM),
            pl.BlockSpec((tm, N_SLOTS), lambda i: (i, 0), memory_space=pltpu.SMEM),
            pl.BlockSpec((tm, D), lambda i: (i, 0)),
            pl.BlockSpec(tab.shape, lambda i: (0, 0), pipeline_mode=pl.Buffered(1)),
        ],
        out_specs=pl.BlockSpec((tm, D), lambda i: (i, 0)),
        out_shape=jax.ShapeDtypeStruct((T, D), F32),
        compiler_params=pltpu.CompilerParams(vmem_limit_bytes=VMEM_LIMIT_TABLE),
        name="peer_v",
    )(idx, act, x1, tab)


def _rope_tables(seq):
    half = HEAD_DIM // 2
    freqs = ROPE_THETA ** (-jnp.arange(0, HEAD_DIM, 2, dtype=F32) / HEAD_DIM)
    ang = jnp.arange(seq, dtype=F32)[:, None] * freqs[None, :]
    lane = jnp.arange(LANES)
    cos = jnp.cos(ang)[:, lane % half]
    sin = jnp.sin(ang)[:, lane % half]
    upper = (lane % HEAD_DIM) >= half
    sina = jnp.where(upper[None, :], sin, 0.0)
    sinb = jnp.where(upper[None, :], 0.0, -sin)
    return cos, sina, sinb


def _pick_tile(n, want):
    t = min(want, n)
    while n % t:
        t //= 2
    return t


def kernel(x, attn_norm_g, w_in, q_norm_g, k_norm_g, pool_w, pool_scale, w_out, ffn_norm_g,
           peer_w_query, peer_sub_keys, peer_u, peer_v):
    B, S, D = x.shape
    T = B * S
    depth = attn_norm_g.shape[0]
    assert S % (DILATIONS[-1] * BLOCK) == 0 and D == 2 * ROW_SUBLANES * LANES
    tm_dense = _pick_tile(S, 512)
    tq = _pick_tile(T, 256)
    tm_peer = _pick_tile(T, 128)
    cos, sina, sinb = _rope_tables(S)
    head_id = jnp.arange(D_ATT) // HEAD_DIM
    bd = (head_id[:, None] == head_id[None, :]).astype(BF16) * (1.0 / HEAD_DIM)
    reps = D_ATT // HEAD_DIM

    x2 = x.reshape(T, D)
    for l in range(depth):
        qg = jnp.tile(q_norm_g[l], reps)[None, :] * (HEAD_DIM ** -0.5)
        kg = jnp.tile(k_norm_g[l], reps)[None, :]
        q, k, v, p = _inproj(x2, attn_norm_g[l][None, :], w_in[l].astype(BF16), qg, kg,
                             cos, sina, sinb, bd, S, tm_dense)
        branches = [_attn_branch(q, k, v, B, S, dil) for dil in DILATIONS]
        (o1, l1), (o2, l2), (o3, l3) = branches
        x1, h2, qp = _outproj(x2, o1, o2, o3, l1, l2, l3, p, pool_w[l].astype(BF16),
                              pool_scale[l][None, :], w_out[l].astype(BF16),
                              ffn_norm_g[l][None, :], peer_w_query[l].astype(BF16), S, tm_dense)
        keys = peer_sub_keys[l].reshape(2 * PEER_HEADS, N_KEYS, PEER_HALF).astype(BF16)
        idx, gate = _topk(qp, keys, tq)
        act = _peer_u(idx, gate, h2, _pack_table(peer_u[l]), tm_peer)
        x2 = _peer_v(idx, act, x1, _pack_table(peer_v[l]), tm_peer)
    return x2.reshape(B, S, D)
```

```python
import functools
import math

import jax
import jax.numpy as jnp
from jax import lax
from jax.experimental import pallas as pl
from jax.experimental.pallas import tpu as pltpu

N_ATT_HEADS = 8
HEAD_DIM = 64
D_ATT = N_ATT_HEADS * HEAD_DIM
POOL_WINDOWS = (2, 4, 8, 16)
POOL_GROUP = 128
POOL_HALO = 16
DILATIONS = (1, 4, 16)
BLOCK = 128
ROPE_THETA = 10000.0
EPS = 1e-6
N_KEYS = 128
PEER_HEADS = 8
PEER_HALF = 128
PEER_TOPK = 16
N_SLOTS = PEER_HEADS * PEER_TOPK

LANES = 128
SUBLANES = 8
ROW_SUBLANES = 4
HALF_D = ROW_SUBLANES * LANES
CHUNKS = 2 * ROW_SUBLANES
QUAD = 4
VMEM_LIMIT_TABLE = 50 * 1024 * 1024
VMEM_LIMIT_DENSE = 48 * 1024 * 1024

NEG = -1e30
BF16 = jnp.bfloat16
F32 = jnp.float32


def _nt_dot(a, b):
    return lax.dot_general(a, b, (((1,), (1,)), ((), ())), preferred_element_type=F32)


def _split_dot(x, w_bf16):
    hi = x.astype(BF16)
    lo = (x - hi.astype(F32)).astype(BF16)
    return (jnp.dot(hi, w_bf16, preferred_element_type=F32)
            + jnp.dot(lo, w_bf16, preferred_element_type=F32))


def _inproj_kernel(x_ref, g_ref, w_ref, qg_ref, kg_ref, cos_ref, sina_ref, sinb_ref, bd_ref,
                   q_ref, k_ref, v_ref, p_ref):
    x = x_ref[...]
    ms = jnp.mean(x * x, axis=-1, keepdims=True)
    h = (x * lax.rsqrt(ms + EPS) * g_ref[...]).astype(BF16)
    proj = jnp.dot(h, w_ref[...], preferred_element_type=F32)
    cos = jnp.concatenate([cos_ref[...]] * (D_ATT // LANES), axis=1)
    sina = jnp.concatenate([sina_ref[...]] * (D_ATT // LANES), axis=1)
    sinb = jnp.concatenate([sinb_ref[...]] * (D_ATT // LANES), axis=1)
    bd = bd_ref[...]

    def head_norm_rope(z, gain):
        msh = _split_dot(z * z, bd)
        z = z * lax.rsqrt(msh + EPS) * gain
        half = HEAD_DIM // 2
        return (z * cos + pltpu.roll(z, half, 1) * sina
                + pltpu.roll(z, D_ATT - half, 1) * sinb)

    q_ref[...] = head_norm_rope(proj[:, :D_ATT], qg_ref[...]).astype(BF16)
    k_ref[...] = head_norm_rope(proj[:, D_ATT:2 * D_ATT], kg_ref[...]).astype(BF16)
    v_ref[...] = proj[:, 2 * D_ATT:3 * D_ATT].astype(BF16)
    p_ref[...] = proj[:, 3 * D_ATT:]


def _inproj(x2, g, w_bf, qg, kg, cos, sina, sinb, bd, seq, tm):
    T, D = x2.shape
    n_out = w_bf.shape[1]
    d_pool = n_out - 3 * D_ATT
    tiles_per_seq = seq // tm
    row = lambda i: (i, 0)
    fixed = lambda i: (0, 0)
    pos = lambda i: (i % tiles_per_seq, 0)
    return pl.pallas_call(
        _inproj_kernel,
        grid=(T // tm,),
        in_specs=[
            pl.BlockSpec((tm, D), row),
            pl.BlockSpec((1, D), fixed),
            pl.BlockSpec((D, n_out), fixed),
            pl.BlockSpec((1, D_ATT), fixed),
            pl.BlockSpec((1, D_ATT), fixed),
            pl.BlockSpec((tm, LANES), pos),
            pl.BlockSpec((tm, LANES), pos),
            pl.BlockSpec((tm, LANES), pos),
            pl.BlockSpec((D_ATT, D_ATT), fixed),
        ],
        out_specs=[
            pl.BlockSpec((tm, D_ATT), row),
            pl.BlockSpec((tm, D_ATT), row),
            pl.BlockSpec((tm, D_ATT), row),
            pl.BlockSpec((tm, d_pool), row),
        ],
        out_shape=[
            jax.ShapeDtypeStruct((T, D_ATT), BF16),
            jax.ShapeDtypeStruct((T, D_ATT), BF16),
            jax.ShapeDtypeStruct((T, D_ATT), BF16),
            jax.ShapeDtypeStruct((T, d_pool), F32),
        ],
        compiler_params=pltpu.CompilerParams(vmem_limit_bytes=VMEM_LIMIT_DENSE),
        name="inproj",
    )(x2, g, w_bf, qg, kg, cos, sina, sinb, bd)


def _attn_kernel(q_ref, kp_ref, kc_ref, vp_ref, vc_ref, o_ref, l_ref, *, rq):
    n = pl.program_id(2)
    lane = lax.broadcasted_iota(jnp.int32, (BLOCK, LANES), 1)
    a = lax.broadcasted_iota(jnp.int32, (BLOCK, 2 * BLOCK), 0)
    c = lax.broadcasted_iota(jnp.int32, (BLOCK, 2 * BLOCK), 1)
    band = (c >= a) & (c <= a + BLOCK)
    has_prev = (n > 0) | (c >= BLOCK)
    zero = jnp.zeros((), BF16)
    for i in range(rq // BLOCK):
        q = q_ref[i * BLOCK:(i + 1) * BLOCK, :]
        if i == 0:
            keys = jnp.concatenate([kp_ref[...], kc_ref[0:BLOCK, :]], axis=0)
            vals = jnp.concatenate([vp_ref[...], vc_ref[0:BLOCK, :]], axis=0)
            valid = band & has_prev
        else:
            keys = kc_ref[(i - 1) * BLOCK:(i + 1) * BLOCK, :]
            vals = vc_ref[(i - 1) * BLOCK:(i + 1) * BLOCK, :]
            valid = band
        outs, lses = [], []
        for hh in range(2):
            in_head = (lane >= hh * HEAD_DIM) & (lane < (hh + 1) * HEAD_DIM)
            s = _nt_dot(jnp.where(in_head, q, zero), keys)
            s = jnp.where(valid, s, NEG)
            m = jnp.max(s, axis=1, keepdims=True)
            p = jnp.exp(s - m)
            l = jnp.sum(p, axis=1, keepdims=True)
            o = jnp.dot(p.astype(BF16), vals, preferred_element_type=F32)
            outs.append(o / l)
            lses.append(jnp.broadcast_to(m + jnp.log(l), (BLOCK, LANES)))
        first = lane < HEAD_DIM
        o_ref[i * BLOCK:(i + 1) * BLOCK, :] = jnp.where(first, outs[0], outs[1])
        l_ref[i * BLOCK:(i + 1) * BLOCK, :] = jnp.where(first, lses[0], lses[1])


def _attn_branch(q, k, v, batch, seq, dil):
    L = seq // dil
    width = dil * D_ATT
    rq = min(4 * BLOCK, L)
    nq = L // rq
    blocks_per_rq = rq // BLOCK
    view = lambda t: t.reshape(batch, L, width)
    cur = lambda b, j, n: (b, n, j)
    prev = lambda b, j, n: (b, jnp.maximum(n * blocks_per_rq - 1, 0), j)
    o, lse = pl.pallas_call(
        functools.partial(_attn_kernel, rq=rq),
        grid=(batch, width // LANES, nq),
        in_specs=[
            pl.BlockSpec((None, rq, LANES), cur),
            pl.BlockSpec((None, BLOCK, LANES), prev),
            pl.BlockSpec((None, rq, LANES), cur),
            pl.BlockSpec((None, BLOCK, LANES), prev),
            pl.BlockSpec((None, rq, LANES), cur),
        ],
        out_specs=[
            pl.BlockSpec((None, rq, LANES), cur),
            pl.BlockSpec((None, rq, LANES), cur),
        ],
        out_shape=[
            jax.ShapeDtypeStruct((batch, L, width), F32),
            jax.ShapeDtypeStruct((batch, L, width), F32),
        ],
        name=f"attn_d{dil}",
    )(view(q), view(k), view(k), view(v), view(v))
    return o.reshape(batch * seq, D_ATT), lse.reshape(batch * seq, D_ATT)


def _outproj_kernel(x_ref, o1_ref, o2_ref, o3_ref, l1_ref, l2_ref, l3_ref, p_ref, halo_ref,
                    pw_ref, ps_ref, wo_ref, g2_ref, wq_ref,
                    x1_ref, h2_ref, qp_ref, buf_ref, *, tm, tiles_per_seq):
    i = pl.program_id(0)
    l1, l2, l3 = l1_ref[...], l2_ref[...], l3_ref[...]
    m = jnp.maximum(jnp.maximum(l1, l2), l3)
    w1, w2, w3 = jnp.exp(l1 - m), jnp.exp(l2 - m), jnp.exp(l3 - m)
    att = (w1 * o1_ref[...] + w2 * o2_ref[...] + w3 * o3_ref[...]) / (w1 + w2 + w3)

    seq_tile = i % tiles_per_seq
    buf_ref[0:POOL_HALO, :] = jnp.where(seq_tile > 0, halo_ref[...], 0.0)
    buf_ref[POOL_HALO:POOL_HALO + tm, :] = p_ref[...]
    pos = seq_tile * tm + lax.broadcasted_iota(jnp.int32, (tm, 1), 0)
    pooled = []
    for g, w in enumerate(POOL_WINDOWS):
        cols = slice(g * POOL_GROUP, (g + 1) * POOL_GROUP)
        acc = buf_ref[POOL_HALO:POOL_HALO + tm, cols]
        for j in range(1, w):
            acc = acc + buf_ref[POOL_HALO - j:POOL_HALO - j + tm, cols]
        cnt = jnp.minimum(pos + 1, w).astype(F32)
        centred = acc / cnt - buf_ref[POOL_HALO:POOL_HALO + tm, cols]
        pooled.append(jnp.dot(centred.astype(BF16), pw_ref[g], preferred_element_type=F32))
    pool = jnp.concatenate(pooled, axis=1) * ps_ref[...]

    mixed = jnp.concatenate([att, pool], axis=1).astype(BF16)
    x1 = x_ref[...] + jnp.dot(mixed, wo_ref[...], preferred_element_type=F32)
    x1_ref[...] = x1
    ms = jnp.mean(x1 * x1, axis=-1, keepdims=True)
    h2 = x1 * lax.rsqrt(ms + EPS) * g2_ref[...]
    h2_ref[...] = h2
    qp_ref[...] = jnp.dot(h2.astype(BF16), wq_ref[...], preferred_element_type=F32).astype(BF16)


def _outproj(x2, o1, o2, o3, l1, l2, l3, p, pw_bf, ps, wo_bf, g2, wq_bf, seq, tm):
    T, D = x2.shape
    d_pool = p.shape[1]
    nq = wq_bf.shape[1]
    tiles_per_seq = seq // tm
    row = lambda i: (i, 0)
    fixed = lambda i: (0, 0)
    halo = lambda i: (jnp.maximum(i * (tm // POOL_HALO) - 1, 0), 0)
    return pl.pallas_call(
        functools.partial(_outproj_kernel, tm=tm, tiles_per_seq=tiles_per_seq),
        grid=(T // tm,),
        in_specs=[pl.BlockSpec((tm, D), row)]
        + [pl.BlockSpec((tm, D_ATT), row)] * 6
        + [
            pl.BlockSpec((tm, d_pool), row),
            pl.BlockSpec((POOL_HALO, d_pool), halo),
            pl.BlockSpec((len(POOL_WINDOWS), POOL_GROUP, POOL_GROUP), lambda i: (0, 0, 0)),
            pl.BlockSpec((1, d_pool), fixed),
            pl.BlockSpec((D_ATT + d_pool, D), fixed),
            pl.BlockSpec((1, D), fixed),
            pl.BlockSpec((D, nq), fixed),
        ],
        out_specs=[
            pl.BlockSpec((tm, D), row),
            pl.BlockSpec((tm, D), row),
            pl.BlockSpec((tm, nq), row),
        ],
        out_shape=[
            jax.ShapeDtypeStruct((T, D), F32),
            jax.ShapeDtypeStruct((T, D), F32),
            jax.ShapeDtypeStruct((T, nq), BF16),
        ],
        scratch_shapes=[pltpu.VMEM((POOL_HALO + tm, d_pool), F32)],
        compiler_params=pltpu.CompilerParams(vmem_limit_bytes=VMEM_LIMIT_DENSE),
        name="outproj",
    )(x2, o1, o2, o3, l1, l2, l3, p, p, pw_bf, ps, wo_bf, g2, wq_bf)


def _take_top(vals, n_take):
    rows = vals.shape[0]
    row_id = lax.broadcasted_iota(jnp.int32, vals.shape, 0)
    top_v, top_i = [], []
    for _ in range(n_take):
        m = jnp.max(vals, axis=0, keepdims=True)
        pos = jnp.min(jnp.where(vals == m, row_id, rows), axis=0, keepdims=True)
        top_v.append(m)
        top_i.append(pos)
        vals = jnp.where(row_id == pos, -jnp.inf, vals)
    return jnp.concatenate(top_v, axis=0), jnp.concatenate(top_i, axis=0)


def _select_rows(table, pos):
    r = table.shape[0]
    row_id = lax.broadcasted_iota(jnp.int32, table.shape, 0)
    out = []
    for j in range(pos.shape[0]):
        hit = row_id == pos[j:j + 1, :]
        out.append(jnp.sum(jnp.where(hit, table, 0), axis=0, keepdims=True))
    return jnp.concatenate(out, axis=0)


def _topk_kernel(qp_ref, keys_ref, idx_ref, gate_ref):
    idx_rows, gate_rows = [], []
    for h in range(PEER_HEADS):
        tops = []
        for half in range(2):
            col = (2 * h + half) * PEER_HALF
            s = _nt_dot(keys_ref[2 * h + half], qp_ref[:, col:col + PEER_HALF])
            tops.append(_take_top(s, PEER_TOPK))
        (s1, i1), (s2, i2) = tops
        cand = jnp.concatenate([s1[a:a + 1, :] + s2 for a in range(PEER_TOPK)], axis=0)
        best, pos = _take_top(cand, PEER_TOPK)
        expert = (_select_rows(i1, pos >> 4) * N_KEYS
                  + _select_rows(i2, pos & (PEER_TOPK - 1)))
        e = jnp.exp(best - best[0:1, :])
        gate_rows.append(e / jnp.sum(e, axis=0, keepdims=True))
        idx_rows.append(expert * ROW_SUBLANES)
    idx_ref[...] = jnp.concatenate(idx_rows, axis=0).T
    gate_ref[...] = jnp.concatenate(gate_rows, axis=0).T


def _topk(qp, keys_bf, tq):
    T, nq = qp.shape
    return pl.pallas_call(
        _topk_kernel,
        grid=(T // tq,),
        in_specs=[
            pl.BlockSpec((tq, nq), lambda i: (i, 0)),
            pl.BlockSpec((2 * PEER_HEADS, N_KEYS, PEER_HALF), lambda i: (0, 0, 0)),
        ],
        out_specs=[
            pl.BlockSpec((tq, N_SLOTS), lambda i: (i, 0)),
            pl.BlockSpec((tq, N_SLOTS), lambda i: (i, 0)),
        ],
        out_shape=[
            jax.ShapeDtypeStruct((T, N_SLOTS), jnp.int32),
            jax.ShapeDtypeStruct((T, N_SLOTS), F32),
        ],
        name="topk",
    )(qp, keys_bf)


def _pack_table(w):
    n, d = w.shape
    bits = lax.bitcast_convert_type(w.astype(BF16), jnp.uint16).astype(jnp.uint32)
    words = bits[:, :d // 2] | (bits[:, d // 2:] << 16)
    return words.reshape(n * ROW_SUBLANES, LANES)


def _gelu(a):
    return 0.5 * a * (1.0 + lax.erf(a * (1.0 / math.sqrt(2.0))))


def _chunk_diag():
    r = lax.broadcasted_iota(jnp.int32, (CHUNKS, N_SLOTS * CHUNKS), 0)
    c = lax.broadcasted_iota(jnp.int32, (CHUNKS, N_SLOTS * CHUNKS), 1)
    return (c & (CHUNKS - 1)) == r


def _gather_quad(idx_ref, tab_ref, q, dst_refs):
    for k in range(N_SLOTS):
        for d in range(QUAD):
            e = pl.multiple_of(idx_ref[QUAD * q + d, k], ROW_SUBLANES)
            dst_refs[d][ROW_SUBLANES * k:ROW_SUBLANES * (k + 1), :] = tab_ref[pl.ds(e, ROW_SUBLANES), :]


def _quad_pipeline(tm, idx_ref, nxt_ref, tab_ref, finish, bufs_a, bufs_b):
    nq = tm // QUAD

    @pl.when(pl.program_id(0) == 0)
    def _():
        _gather_quad(idx_ref, tab_ref, 0, bufs_a)

    for q in range(nq):
        this, other = (bufs_a, bufs_b) if q % 2 == 0 else (bufs_b, bufs_a)
        if q + 1 < nq:
            _gather_quad(idx_ref, tab_ref, q + 1, other)
        else:
            _gather_quad(nxt_ref, tab_ref, 0, other)
        for d in range(QUAD):
            finish(QUAD * q + d, this[d])


def _peer_specs(T, tm, D, tab, const_shape):
    nxt = lambda i: (jnp.minimum((i + 1) * (tm // SUBLANES), T // SUBLANES - 1), 0)
    return [
        pl.BlockSpec((tm, N_SLOTS), lambda i: (i, 0), memory_space=pltpu.SMEM),
        pl.BlockSpec((SUBLANES, N_SLOTS), nxt, memory_space=pltpu.SMEM),
        pl.BlockSpec((tm, N_SLOTS), lambda i: (i, 0)),
        pl.BlockSpec((tm, D), lambda i: (i, 0)),
        pl.BlockSpec(tab.shape, lambda i: (0, 0), pipeline_mode=pl.Buffered(1)),
        pl.BlockSpec(const_shape, lambda i: (0, 0)),
    ]


def _peer_u_kernel(idx_ref, nxt_ref, gate_ref, h_ref, tab_ref, fold_ref, act_ref, *scratch, tm):
    bufs_a, bufs_b, z_ref = scratch[:QUAD], scratch[QUAD:2 * QUAD], scratch[2 * QUAD]
    diag = _chunk_diag()

    def finish(t, src_ref):
        rows = pltpu.bitcast(src_ref[...], BF16)
        hrow = h_ref[t:t + 1, :]
        h8 = jnp.concatenate([hrow[:, HALF_D * s + LANES * j:HALF_D * s + LANES * (j + 1)]
                              for j in range(ROW_SUBLANES) for s in range(2)], axis=0)
        hh = h8.astype(BF16)
        hl = (h8 - hh.astype(F32)).astype(BF16)
        o16 = _nt_dot(jnp.concatenate([hh, hl], axis=0), rows)
        o8 = o16[:CHUNKS] + o16[CHUNKS:]
        z_ref[t:t + 1, :] = jnp.sum(jnp.where(diag, o8, 0.0), axis=0, keepdims=True)

    _quad_pipeline(tm, idx_ref, nxt_ref, tab_ref, finish, bufs_a, bufs_b)
    a = _split_dot(z_ref[...], fold_ref[...])
    act_ref[...] = _gelu(a) * gate_ref[...]


def _peer_u(idx, gate, h2, tab, tm):
    T, D = h2.shape
    fold = (jnp.arange(N_SLOTS * CHUNKS)[:, None] // CHUNKS == jnp.arange(N_SLOTS)[None, :]).astype(BF16)
    return pl.pallas_call(
        functools.partial(_peer_u_kernel, tm=tm),
        grid=(T // tm,),
        in_specs=_peer_specs(T, tm, D, tab, fold.shape),
        out_specs=pl.BlockSpec((tm, N_SLOTS), lambda i: (i, 0)),
        out_shape=jax.ShapeDtypeStruct((T, N_SLOTS), F32),
        scratch_shapes=[pltpu.VMEM((N_SLOTS * ROW_SUBLANES, LANES), jnp.uint32)] * (2 * QUAD)
        + [pltpu.VMEM((tm, N_SLOTS * CHUNKS), F32)],
        compiler_params=pltpu.CompilerParams(dimension_semantics=("arbitrary",),
                                             vmem_limit_bytes=VMEM_LIMIT_TABLE),
        name="peer_u",
    )(idx, idx, gate, h2, tab, fold)


def _peer_v_kernel(idx_ref, nxt_ref, act_ref, x_ref, tab_ref, spread_ref, out_ref, *scratch, tm):
    bufs_a, bufs_b = scratch[:QUAD], scratch[QUAD:2 * QUAD]
    hi_ref, lo_ref = scratch[2 * QUAD], scratch[2 * QUAD + 1]
    act = act_ref[...]
    hi = act.astype(BF16)
    lo = (act - hi.astype(F32)).astype(BF16)
    hi_ref[...] = jnp.dot(hi, spread_ref[...], preferred_element_type=F32)
    lo_ref[...] = jnp.dot(lo, spread_ref[...], preferred_element_type=F32)
    diag = _chunk_diag()

    def finish(t, src_ref):
        rows = pltpu.bitcast(src_ref[...], BF16)
        ah = jnp.where(diag, hi_ref[t:t + 1, :], 0.0).astype(BF16)
        al = jnp.where(diag, lo_ref[t:t + 1, :], 0.0).astype(BF16)
        o16 = jnp.dot(jnp.concatenate([ah, al], axis=0), rows, preferred_element_type=F32)
        o8 = o16[:CHUNKS] + o16[CHUNKS:]
        y = jnp.concatenate([o8[2 * j + s:2 * j + s + 1, :] for s in range(2) for j in range(ROW_SUBLANES)],
                            axis=1)
        out_ref[t:t + 1, :] = x_ref[t:t + 1, :] + y

    _quad_pipeline(tm, idx_ref, nxt_ref, tab_ref, finish, bufs_a, bufs_b)


def _peer_v(idx, act, x1, tab, tm):
    T, D = x1.shape
    spread = (jnp.arange(N_SLOTS)[:, None] == jnp.arange(N_SLOTS * CHUNKS)[None, :] // CHUNKS).astype(BF16)
    return pl.pallas_call(
        functools.partial(_peer_v_kernel, tm=tm),
        grid=(T // tm,),
        in_specs=_peer_specs(T, tm, D, tab, spread.shape),
        out_specs=pl.BlockSpec((tm, D), lambda i: (i, 0)),
        out_shape=jax.ShapeDtypeStruct((T, D), F32),
        scratch_shapes=[pltpu.VMEM((N_SLOTS * ROW_SUBLANES, LANES), jnp.uint32)] * (2 * QUAD)
        + [pltpu.VMEM((tm, N_SLOTS * CHUNKS), F32)] * 2,
        compiler_params=pltpu.CompilerParams(dimension_semantics=("arbitrary",),
                                             vmem_limit_bytes=VMEM_LIMIT_TABLE),
        name="peer_v",
    )(idx, idx, act, x1, tab, spread)


def _rope_tables(seq):
    half = HEAD_DIM // 2
    freqs = ROPE_THETA ** (-jnp.arange(0, HEAD_DIM, 2, dtype=F32) / HEAD_DIM)
    ang = jnp.arange(seq, dtype=F32)[:, None] * freqs[None, :]
    lane = jnp.arange(LANES)
    cos = jnp.cos(ang)[:, lane % half]
    sin = jnp.sin(ang)[:, lane % half]
    upper = (lane % HEAD_DIM) >= half
    sina = jnp.where(upper[None, :], sin, 0.0)
    sinb = jnp.where(upper[None, :], 0.0, -sin)
    return cos, sina, sinb


def _pick_tile(n, want):
    t = min(want, n)
    while n % t:
        t //= 2
    return t


def kernel(x, attn_norm_g, w_in, q_norm_g, k_norm_g, pool_w, pool_scale, w_out, ffn_norm_g,
           peer_w_query, peer_sub_keys, peer_u, peer_v):
    B, S, D = x.shape
    T = B * S
    depth = attn_norm_g.shape[0]
    assert S % (DILATIONS[-1] * BLOCK) == 0 and D == 2 * ROW_SUBLANES * LANES
    tm_dense = _pick_tile(S, 512)
    tq = _pick_tile(T, 256)
    tm_peer = _pick_tile(T, 32)
    cos, sina, sinb = _rope_tables(S)
    head_id = jnp.arange(D_ATT) // HEAD_DIM
    bd = (head_id[:, None] == head_id[None, :]).astype(BF16) * (1.0 / HEAD_DIM)
    reps = D_ATT // HEAD_DIM

    x2 = x.reshape(T, D)
    for l in range(depth):
        qg = jnp.tile(q_norm_g[l], reps)[None, :] * (HEAD_DIM ** -0.5)
        kg = jnp.tile(k_norm_g[l], reps)[None, :]
        q, k, v, p = _inproj(x2, attn_norm_g[l][None, :], w_in[l].astype(BF16), qg, kg,
                             cos, sina, sinb, bd, S, tm_dense)
        branches = [_attn_branch(q, k, v, B, S, dil) for dil in DILATIONS]
        (o1, l1), (o2, l2), (o3, l3) = branches
        x1, h2, qp = _outproj(x2, o1, o2, o3, l1, l2, l3, p, pool_w[l].astype(BF16),
                              pool_scale[l][None, :], w_out[l].astype(BF16),
                              ffn_norm_g[l][None, :], peer_w_query[l].astype(BF16), S, tm_dense)
        keys = peer_sub_keys[l].reshape(2 * PEER_HEADS, N_KEYS, PEER_HALF).astype(BF16)
        idx, gate = _topk(qp, keys, tq)
        act = _peer_u(idx, gate, h2, _pack_table(peer_u[l]), tm_peer)
        x2 = _peer_v(idx, act, x1, _pack_table(peer_v[l]), tm_peer)
    return x2.reshape(B, S, D)
```
